```python
import jax, jax.numpy as jnp
from jax import lax
import numpy as np

D_MODEL = 1024
BATCH = 8
SEQ = 8192
DEPTH = 4

CHUNK = 128
N_MIXERS = 2
N_SGU_LAYERS = (DEPTH + 1) // 2
N_RET_LAYERS = DEPTH // 2
EPS = 1e-6
SGU_WIDTH = D_MODEL
SGU_GROUPS = 8
SGU_GROUP_DIM = SGU_WIDTH // SGU_GROUPS
RET_HEADS = 4
RET_QK_DIM = D_MODEL // RET_HEADS
RET_V_DIM = 2 * D_MODEL // RET_HEADS
RET_QK_WIDTH = RET_HEADS * RET_QK_DIM
RET_V_WIDTH = RET_HEADS * RET_V_DIM
RET_IN_WIDTH = 2 * RET_QK_WIDTH + 2 * RET_V_WIDTH
ROPE_BASE = 10000.0
FFN_HIDDEN = -(-8 * D_MODEL // (3 * 256)) * 256

kernel_name = "hybrid_sgu_retention_trunk"


def rmsnorm(x, g):
    xf = x.astype(jnp.float32)
    y = xf * lax.rsqrt(jnp.mean(xf * xf, axis=-1, keepdims=True) + EPS)
    return (y * g.astype(jnp.float32)).astype(x.dtype)


def layernorm(x, g, b):
    xf = x.astype(jnp.float32)
    mu = jnp.mean(xf, axis=-1, keepdims=True)
    var = jnp.mean(jnp.square(xf - mu), axis=-1, keepdims=True)
    y = (xf - mu) * lax.rsqrt(var + EPS)
    return (y * g.astype(jnp.float32) + b.astype(jnp.float32)).astype(x.dtype)


def sgu_mixer(h, w_in, w_s, b_s, ln_g, ln_b, w_out):
    B, S, _ = h.shape
    nc = S // CHUNK
    z = jax.nn.gelu(h @ w_in)
    u, v = jnp.split(z, 2, axis=-1)
    v = layernorm(v, ln_g, ln_b)
    v = v.reshape(B, nc, CHUNK, SGU_GROUPS, SGU_GROUP_DIM)
    causal = jnp.tril(jnp.ones((CHUNK, CHUNK), dtype=bool))
    w = jnp.where(causal[None], w_s, jnp.zeros_like(w_s))
    mixed = jnp.einsum('gts,bcsgd->bctgd', w, v)
    mixed = mixed + b_s.T[None, None, :, :, None]
    mixed = mixed.reshape(B, S, SGU_WIDTH)
    return (u * mixed) @ w_out


def rotary(x, positions):
    half = x.shape[-1] // 2
    inv = 1.0 / (ROPE_BASE ** (jnp.arange(half, dtype=jnp.float32) / half))
    ang = positions.astype(jnp.float32)[..., None] * inv
    cos = jnp.cos(ang)[:, :, None, :]
    sin = jnp.sin(ang)[:, :, None, :]
    xf = x.astype(jnp.float32)
    x1, x2 = xf[..., :half], xf[..., half:]
    return jnp.concatenate([x1 * cos - x2 * sin, x1 * sin + x2 * cos], axis=-1)


def retention(h, positions, w_in, gn_g, w_out):
    B, S, _ = h.shape
    nc = S // CHUNK
    proj = h @ w_in
    q, k, v, g = jnp.split(proj, [RET_QK_WIDTH, 2 * RET_QK_WIDTH,
                                  2 * RET_QK_WIDTH + RET_V_WIDTH], axis=-1)
    q = rotary(q.reshape(B, S, RET_HEADS, RET_QK_DIM), positions)
    k = rotary(k.reshape(B, S, RET_HEADS, RET_QK_DIM), positions) * (RET_QK_DIM ** -0.5)
    v = v.reshape(B, S, RET_HEADS, RET_V_DIM).astype(jnp.float32)

    def to_chunks(t):
        return t.reshape(B, nc, CHUNK, RET_HEADS, -1).transpose(1, 0, 3, 2, 4)

    log_gamma = jnp.log1p(-(2.0 ** (-5.0 - jnp.arange(RET_HEADS, dtype=jnp.float32))))
    idx = jnp.arange(CHUNK, dtype=jnp.float32)
    diff = idx[:, None] - idx[None, :]
    decay_inner = jnp.where(diff >= 0,
                            jnp.exp(log_gamma[:, None, None] * jnp.maximum(diff, 0.0)),
                            0.0)
    cross_decay = jnp.exp(log_gamma[:, None] * (idx + 1.0))
    state_decay = jnp.exp(log_gamma[:, None] * (CHUNK - 1.0 - idx))
    chunk_decay = jnp.exp(log_gamma * CHUNK)

    def step(state, qkv):
        qc, kc, vc = qkv
        scores = jnp.einsum('bhtd,bhsd->bhts', qc, kc) * decay_inner[None]
        inner = jnp.einsum('bhts,bhsv->bhtv', scores, vc)
        cross = jnp.einsum('bhtd,bhdv->bhtv', qc, state) * cross_decay[None, :, :, None]
        new_state = state * chunk_decay[None, :, None, None] + jnp.einsum(
            'bhsd,bhsv->bhdv', kc * state_decay[None, :, :, None], vc)
        return new_state, inner + cross

    state0 = jnp.zeros((B, RET_HEADS, RET_QK_DIM, RET_V_DIM), jnp.float32)
    _, o = lax.scan(step, state0, (to_chunks(q), to_chunks(k), to_chunks(v)))
    o = o.transpose(1, 0, 3, 2, 4).reshape(B, S, RET_HEADS, RET_V_DIM)
    o = o * lax.rsqrt(jnp.mean(o * o, axis=-1, keepdims=True) + EPS)
    o = o.reshape(B, S, RET_V_WIDTH) * gn_g.astype(jnp.float32)
    return (o.astype(h.dtype) * jax.nn.silu(g)) @ w_out


def swiglu(h, w_gu, w_down):
    a, b = jnp.split(h @ w_gu, 2, axis=-1)
    return (jax.nn.silu(a) * b) @ w_down


def setup_inputs(seed: int = 0) -> dict:
    key = jax.random.key(seed)
    ks = jax.random.split(key, 16)
    f32 = jnp.float32

    def nrm(k, shape, scale):
        return jax.random.normal(k, shape, f32) * scale

    na, nb = N_SGU_LAYERS, N_RET_LAYERS
    x = jax.random.normal(ks[0], (BATCH, SEQ, D_MODEL), f32)
    positions = jnp.broadcast_to(jnp.arange(SEQ, dtype=jnp.int32), (BATCH, SEQ))
    return {
        "x": x,
        "positions": positions,
        "mix_norm_g": 1.0 + nrm(ks[1], (DEPTH, D_MODEL), 0.02),
        "ffn_norm_g": 1.0 + nrm(ks[2], (DEPTH, D_MODEL), 0.02),
        "final_norm_g": 1.0 + nrm(ks[3], (D_MODEL,), 0.02),
        "sgu_w_in": nrm(ks[4], (na, D_MODEL, 2 * SGU_WIDTH), D_MODEL ** -0.5),
        "sgu_w_s": nrm(ks[5], (na, SGU_GROUPS, CHUNK, CHUNK), CHUNK ** -0.5),
        "sgu_b_s": 1.0 + nrm(ks[6], (na, SGU_GROUPS, CHUNK), 0.02),
        "sgu_ln_g": 1.0 + nrm(ks[7], (na, SGU_WIDTH), 0.02),
        "sgu_ln_b": nrm(ks[8], (na, SGU_WIDTH), 0.02),
        "sgu_w_out": nrm(ks[9], (na, SGU_WIDTH, D_MODEL), SGU_WIDTH ** -0.5),
        "ret_w_in": nrm(ks[10], (nb, D_MODEL, RET_IN_WIDTH), D_MODEL ** -0.5),
        "ret_gn_g": 1.0 + nrm(ks[11], (nb, RET_V_WIDTH), 0.02),
        "ret_w_out": nrm(ks[12], (nb, RET_V_WIDTH, D_MODEL), RET_V_WIDTH ** -0.5),
        "ffn_w_gu": nrm(ks[13], (DEPTH, D_MODEL, 2 * FFN_HIDDEN), D_MODEL ** -0.5),
        "ffn_w_down": nrm(ks[14], (DEPTH, FFN_HIDDEN, D_MODEL), FFN_HIDDEN ** -0.5),
    }


def reference(x, positions, mix_norm_g, ffn_norm_g, final_norm_g,
              sgu_w_in, sgu_w_s, sgu_b_s, sgu_ln_g, sgu_ln_b, sgu_w_out,
              ret_w_in, ret_gn_g, ret_w_out, ffn_w_gu, ffn_w_down):
    h = x
    for i in range(DEPTH):
        hn = rmsnorm(h, mix_norm_g[i])
        j = i // N_MIXERS
        if i % N_MIXERS == 0:
            h = h + sgu_mixer(hn, sgu_w_in[j], sgu_w_s[j], sgu_b_s[j],
                              sgu_ln_g[j], sgu_ln_b[j], sgu_w_out[j])
        else:
            h = h + retention(hn, positions, ret_w_in[j], ret_gn_g[j], ret_w_out[j])
        h = h + swiglu(rmsnorm(h, ffn_norm_g[i]), ffn_w_gu[i], ffn_w_down[i])
    return rmsnorm(h, final_norm_g)
```

```python
from functools import partial

import numpy as np
import jax
import jax.numpy as jnp
from jax import lax
from jax.experimental import pallas as pl
from jax.experimental.pallas import tpu as pltpu

D_MODEL = 1024
CHUNK = 128
EPS = 1e-6
SGU_GROUPS = 8
SGU_GROUP_DIM = D_MODEL // SGU_GROUPS
RET_HEADS = 4
RET_QK_DIM = D_MODEL // RET_HEADS
RET_V_DIM = 2 * D_MODEL // RET_HEADS
RET_QK_WIDTH = RET_HEADS * RET_QK_DIM
RET_V_WIDTH = RET_HEADS * RET_V_DIM
ROPE_HALF = RET_QK_DIM // 2
ROPE_BASE = 10000.0

VMEM_LIMIT_BYTES = 56 * 1024 * 1024
LANES = 128
SUBLANES = 8

FFN_TILE = 512
SGU_TILE = 512
RET_TILE = 512
ROPE_ROWS = SUBLANES * LANES

BF16 = jnp.bfloat16
F32 = jnp.float32


def _mm(a, b):
    return jnp.dot(a, b, preferred_element_type=F32)


def _rmsnorm(x, g):
    return x * lax.rsqrt(jnp.mean(x * x, axis=-1, keepdims=True) + EPS) * g


def _resident(shape):
    zeros = (0,) * len(shape)
    return pl.BlockSpec(shape, lambda *_: zeros, pipeline_mode=pl.Buffered(1))


def _params(n_grid_axes):
    return pltpu.CompilerParams(
        dimension_semantics=("arbitrary",) * n_grid_axes,
        vmem_limit_bytes=VMEM_LIMIT_BYTES,
    )


def _ffn_kernel(h_ref, g_ref, wgu_ref, wd_ref, fg_ref, o_ref, *, hidden, apply_final):
    h = h_ref[...]
    hn = _rmsnorm(h, g_ref[...]).astype(BF16)
    a = _mm(hn, wgu_ref[:, :hidden])
    b = _mm(hn, wgu_ref[:, hidden:])
    act = (jax.nn.silu(a) * b).astype(BF16)
    out = h + _mm(act, wd_ref[...])
    if apply_final:
        out = _rmsnorm(out, fg_ref[...])
    o_ref[...] = out


def _ffn(h2, norm_g, w_gu, w_down, final_g, apply_final):
    n_tok, d = h2.shape
    hidden = w_down.shape[0]
    row = pl.BlockSpec((FFN_TILE, d), lambda i: (i, 0))
    return pl.pallas_call(
        partial(_ffn_kernel, hidden=hidden, apply_final=apply_final),
        grid=(n_tok // FFN_TILE,),
        in_specs=[row, _resident((1, d)), _resident(w_gu.shape), _resident(w_down.shape),
                  _resident((1, d))],
        out_specs=row,
        out_shape=jax.ShapeDtypeStruct(h2.shape, F32),
        compiler_params=_params(1),
        name="ffn",
    )(h2, norm_g.reshape(1, d), w_gu, w_down, final_g.reshape(1, d))


def _sgu_kernel(h_ref, g_ref, win_ref, ws_ref, bst_ref, lng_ref, lnb_ref, wout_ref, o_ref,
                gate_ref):
    width = wout_ref.shape[0]
    h = h_ref[...]
    hn = _rmsnorm(h, g_ref[...]).astype(BF16)
    z = jax.nn.gelu(_mm(hn, win_ref[...]))
    u = z[:, :width]
    v = z[:, width:]
    mu = jnp.mean(v, axis=-1, keepdims=True)
    vc = v - mu
    var = jnp.mean(vc * vc, axis=-1, keepdims=True)
    vn = (vc * lax.rsqrt(var + EPS) * lng_ref[...] + lnb_ref[...]).astype(BF16)

    t_idx = lax.broadcasted_iota(jnp.int32, (CHUNK, CHUNK), 0)
    s_idx = lax.broadcasted_iota(jnp.int32, (CHUNK, CHUNK), 1)
    causal = t_idx >= s_idx
    n_chunks = h.shape[0] // CHUNK
    for g in range(SGU_GROUPS):
        cols = slice(g * SGU_GROUP_DIM, (g + 1) * SGU_GROUP_DIM)
        w = jnp.where(causal, ws_ref[g], 0.0).astype(BF16)
        bias = bst_ref[:, g:g + 1]
        for c in range(n_chunks):
            rows = slice(c * CHUNK, (c + 1) * CHUNK)
            mixed = _mm(w, vn[rows, cols]) + bias
            gate_ref[rows, cols] = (u[rows, cols] * mixed).astype(BF16)
    o_ref[...] = h + _mm(gate_ref[...], wout_ref[...])


def _sgu(h2, norm_g, w_in, w_s, b_s, ln_g, ln_b, w_out):
    n_tok, d = h2.shape
    width = w_out.shape[0]
    row = pl.BlockSpec((SGU_TILE, d), lambda i: (i, 0))
    return pl.pallas_call(
        _sgu_kernel,
        grid=(n_tok // SGU_TILE,),
        in_specs=[row, _resident((1, d)), _resident(w_in.shape), _resident(w_s.shape),
                  _resident((CHUNK, SGU_GROUPS)), _resident((1, width)), _resident((1, width)),
                  _resident(w_out.shape)],
        out_specs=row,
        out_shape=jax.ShapeDtypeStruct(h2.shape, F32),
        scratch_shapes=[pltpu.VMEM((SGU_TILE, width), BF16)],
        compiler_params=_params(1),
        name="sgu",
    )(h2, norm_g.reshape(1, d), w_in, w_s, b_s.T, ln_g.reshape(1, width),
      ln_b.reshape(1, width), w_out)


def _rope_kernel(pos_ref, inv_ref, cos_ref, sin_ref):
    pos = pos_ref[...].astype(F32)
    inv = inv_ref[...]
    eye = (lax.broadcasted_iota(jnp.int32, (LANES, LANES), 0)
           == lax.broadcasted_iota(jnp.int32, (LANES, LANES), 1))
    for r in range(SUBLANES):
        col = jnp.sum(jnp.where(eye, pos[r:r + 1, :], 0.0), axis=-1, keepdims=True)
        ang = col * inv
        cos_ref[r * LANES:(r + 1) * LANES, :] = jnp.cos(ang)
        sin_ref[r * LANES:(r + 1) * LANES, :] = jnp.sin(ang)


def _rope_tables(positions):
    n_tok = positions.size
    inv = (1.0 / (ROPE_BASE ** (jnp.arange(ROPE_HALF, dtype=F32) / ROPE_HALF))).reshape(1, ROPE_HALF)
    table = jax.ShapeDtypeStruct((n_tok, ROPE_HALF), F32)
    out_row = pl.BlockSpec((ROPE_ROWS, ROPE_HALF), lambda i: (i, 0))
    return pl.pallas_call(
        _rope_kernel,
        grid=(n_tok // ROPE_ROWS,),
        in_specs=[pl.BlockSpec((SUBLANES, LANES), lambda i: (i, 0)), _resident((1, ROPE_HALF))],
        out_specs=[out_row, out_row],
        out_shape=[table, table],
        compiler_params=_params(1),
        name="rope_tables",
    )(positions.reshape(n_tok // LANES, LANES), inv)


def _retention_decays():
    log_gamma = np.log1p(-(2.0 ** (-5.0 - np.arange(RET_HEADS, dtype=np.float32)))).astype(np.float32)
    idx = np.arange(CHUNK, dtype=np.float32)
    diff = idx[:, None] - idx[None, :]
    inner = np.where(diff >= 0, np.exp(log_gamma[:, None, None] * np.maximum(diff, 0.0)), 0.0)
    cross = np.exp(log_gamma[:, None] * (idx + 1.0))
    state = np.exp(log_gamma[:, None] * (CHUNK - 1.0 - idx))
    chunk = np.exp(log_gamma * CHUNK)
    return (inner.astype(np.float32), cross.T.astype(np.float32).copy(),
            state.T.astype(np.float32).copy(), [float(x) for x in chunk.astype(np.float32)])


def _ret_kernel(h_ref, cos_ref, sin_ref, g_ref, win_ref, gng_ref, wout_ref,
                dinner_ref, dcross_ref, dstate_ref, o_ref,
                state_ref, q_ref, k_ref, ks_ref, v_ref, ret_ref, *, chunk_decay):
    @pl.when(pl.program_id(1) == 0)
    def _():
        state_ref[...] = jnp.zeros_like(state_ref)

    h = h_ref[0]
    tile = h.shape[0]
    n_chunks = tile // CHUNK
    hn = _rmsnorm(h, g_ref[...]).astype(BF16)
    cos = cos_ref[...]
    sin = sin_ref[...]

    def rotate(x, hd):
        x1 = x[:, hd * RET_QK_DIM:hd * RET_QK_DIM + ROPE_HALF]
        x2 = x[:, hd * RET_QK_DIM + ROPE_HALF:(hd + 1) * RET_QK_DIM]
        return x1 * cos - x2 * sin, x1 * sin + x2 * cos

    q = _mm(hn, win_ref[:, :RET_QK_WIDTH])
    k = _mm(hn, win_ref[:, RET_QK_WIDTH:2 * RET_QK_WIDTH])
    k_scale = RET_QK_DIM ** -0.5
    for hd in range(RET_HEADS):
        lo = hd * RET_QK_DIM
        q1, q2 = rotate(q, hd)
        q_ref[:, lo:lo + ROPE_HALF] = q1.astype(BF16)
        q_ref[:, lo + ROPE_HALF:lo + RET_QK_DIM] = q2.astype(BF16)
        k1, k2 = rotate(k, hd)
        k1 = k1 * k_scale
        k2 = k2 * k_scale
        k_ref[:, lo:lo + ROPE_HALF] = k1.astype(BF16)
        k_ref[:, lo + ROPE_HALF:lo + RET_QK_DIM] = k2.astype(BF16)
        for c in range(n_chunks):
            rows = slice(c * CHUNK, (c + 1) * CHUNK)
            sd = dstate_ref[:, hd:hd + 1]
            ks_ref[rows, lo:lo + ROPE_HALF] = (k1[rows] * sd).astype(BF16)
            ks_ref[rows, lo + ROPE_HALF:lo + RET_QK_DIM] = (k2[rows] * sd).astype(BF16)
    v_ref[...] = _mm(hn, win_ref[:, 2 * RET_QK_WIDTH:2 * RET_QK_WIDTH + RET_V_WIDTH]).astype(BF16)

    for hd in range(RET_HEADS):
        qk = slice(hd * RET_QK_DIM, (hd + 1) * RET_QK_DIM)
        vv = slice(hd * RET_V_DIM, (hd + 1) * RET_V_DIM)
        cd = dcross_ref[:, hd:hd + 1]
        for c in range(n_chunks):
            rows = slice(c * CHUNK, (c + 1) * CHUNK)
            qc = q_ref[rows, qk]
            vc = v_ref[rows, vv]
            scores = lax.dot_general(qc, k_ref[rows, qk], (((1,), (1,)), ((), ())),
                                     preferred_element_type=F32) * dinner_ref[hd]
            inner = _mm(scores.astype(BF16), vc)
            st = state_ref[hd]
            cross = _mm(qc, st.astype(BF16)) * cd
            ret_ref[rows, vv] = inner + cross
            state_ref[hd] = st * chunk_decay[hd] + lax.dot_general(
                ks_ref[rows, qk], vc, (((0,), (0,)), ((), ())), preferred_element_type=F32)

    gate = _mm(hn, win_ref[:, 2 * RET_QK_WIDTH + RET_V_WIDTH:])
    for hd in range(RET_HEADS):
        vv = slice(hd * RET_V_DIM, (hd + 1) * RET_V_DIM)
        o = ret_ref[:, vv]
        o = o * lax.rsqrt(jnp.mean(o * o, axis=-1, keepdims=True) + EPS) * gng_ref[:, vv]
        v_ref[:, vv] = (o * jax.nn.silu(gate[:, vv])).astype(BF16)
    o_ref[0] = h + _mm(v_ref[...], wout_ref[...])


def _retention(h, cos, sin, norm_g, w_in, gn_g, w_out):
    b, s, d = h.shape
    dinner, dcross_t, dstate_t, chunk_decay = _retention_decays()
    tiles_per_seq = s // RET_TILE
    row = pl.BlockSpec((1, RET_TILE, d), lambda i, j: (i, j, 0))
    rope = pl.BlockSpec((RET_TILE, ROPE_HALF), lambda i, j: (i * tiles_per_seq + j, 0))
    return pl.pallas_call(
        partial(_ret_kernel, chunk_decay=chunk_decay),
        grid=(b, tiles_per_seq),
        in_specs=[row, rope, rope, _resident((1, d)), _resident(w_in.shape),
                  _resident((1, RET_V_WIDTH)), _resident(w_out.shape),
                  _resident(dinner.shape), _resident(dcross_t.shape), _resident(dstate_t.shape)],
        out_specs=row,
        out_shape=jax.ShapeDtypeStruct(h.shape, F32),
        scratch_shapes=[
            pltpu.VMEM((RET_HEADS, RET_QK_DIM, RET_V_DIM), F32),
            pltpu.VMEM((RET_TILE, RET_QK_WIDTH), BF16),
            pltpu.VMEM((RET_TILE, RET_QK_WIDTH), BF16),
            pltpu.VMEM((RET_TILE, RET_QK_WIDTH), BF16),
            pltpu.VMEM((RET_TILE, RET_V_WIDTH), BF16),
            pltpu.VMEM((RET_TILE, RET_V_WIDTH), F32),
        ],
        compiler_params=_params(2),
        name="retention",
    )(h, cos, sin, norm_g.reshape(1, d), w_in, gn_g.reshape(1, RET_V_WIDTH), w_out,
      jnp.asarray(dinner), jnp.asarray(dcross_t), jnp.asarray(dstate_t))


def kernel(x, positions, mix_norm_g, ffn_norm_g, final_norm_g, sgu_w_in, sgu_w_s, sgu_b_s,
           sgu_ln_g, sgu_ln_b, sgu_w_out, ret_w_in, ret_gn_g, ret_w_out, ffn_w_gu, ffn_w_down):
    b, s, d = x.shape
    depth = mix_norm_g.shape[0]
    assert d == D_MODEL and s % RET_TILE == 0 and (b * s) % ROPE_ROWS == 0
    cos, sin = _rope_tables(positions)
    h = x
    for i in range(depth):
        j = i // 2
        if i % 2 == 0:
            h = _sgu(h.reshape(b * s, d), mix_norm_g[i], sgu_w_in[j].astype(BF16), sgu_w_s[j],
                     sgu_b_s[j], sgu_ln_g[j], sgu_ln_b[j], sgu_w_out[j].astype(BF16))
        else:
            h = _retention(h.reshape(b, s, d), cos, sin, mix_norm_g[i], ret_w_in[j].astype(BF16),
                           ret_gn_g[j], ret_w_out[j].astype(BF16))
        h = _ffn(h.reshape(b * s, d), ffn_norm_g[i], ffn_w_gu[i].astype(BF16),
                 ffn_w_down[i].astype(BF16), final_norm_g, apply_final=(i == depth - 1))
    return h.reshape(b, s, d)
```

```python
from functools import partial

import numpy as np
import jax
import jax.numpy as jnp
from jax import lax
from jax.experimental import pallas as pl
from jax.experimental.pallas import tpu as pltpu

D_MODEL = 1024
CHUNK = 128
EPS = 1e-6
SGU_GROUPS = 8
SGU_GROUP_DIM = D_MODEL // SGU_GROUPS
RET_HEADS = 4
RET_QK_DIM = D_MODEL // RET_HEADS
RET_V_DIM = 2 * D_MODEL // RET_HEADS
RET_QK_WIDTH = RET_HEADS * RET_QK_DIM
RET_V_WIDTH = RET_HEADS * RET_V_DIM
ROPE_HALF = RET_QK_DIM // 2
ROPE_BASE = 10000.0

VMEM_LIMIT_BYTES = 56 * 1024 * 1024
LANES = 128
SUBLANES = 8

FFN_TILE = 512
SGU_TILE = 512
RET_TILE = 512
RET_CHUNK = 256
ROPE_ROWS = SUBLANES * LANES

BF16 = jnp.bfloat16
F32 = jnp.float32


def _mm(a, b):
    return jnp.dot(a, b, preferred_element_type=F32)


def _rmsnorm(x, g):
    return x * lax.rsqrt(jnp.mean(x * x, axis=-1, keepdims=True) + EPS) * g


def _resident(shape):
    zeros = (0,) * len(shape)
    return pl.BlockSpec(shape, lambda *_: zeros, pipeline_mode=pl.Buffered(1))


def _layer(stacked, layer):
    zeros = (0,) * (stacked.ndim - 1)
    return pl.BlockSpec((None,) + stacked.shape[1:], lambda *_: (layer,) + zeros,
                        pipeline_mode=pl.Buffered(1))


def _rows(stacked):
    return stacked.reshape(stacked.shape[0], 1, stacked.shape[1])


def _params(n_grid_axes):
    return pltpu.CompilerParams(
        dimension_semantics=("arbitrary",) * n_grid_axes,
        vmem_limit_bytes=VMEM_LIMIT_BYTES,
    )


def _ffn_kernel(h_ref, g_ref, wgu_ref, wd_ref, fg_ref, o_ref, *, hidden, apply_final):
    h = h_ref[...]
    hn = _rmsnorm(h, g_ref[...]).astype(BF16)
    a = _mm(hn, wgu_ref[:, :hidden])
    b = _mm(hn, wgu_ref[:, hidden:])
    act = (jax.nn.silu(a) * b).astype(BF16)
    out = h + _mm(act, wd_ref[...])
    if apply_final:
        out = _rmsnorm(out, fg_ref[...])
    o_ref[...] = out


def _ffn(h2, layer, norm_g, w_gu, w_down, final_g, apply_final):
    n_tok, d = h2.shape
    hidden = w_down.shape[1]
    row = pl.BlockSpec((FFN_TILE, d), lambda i: (i, 0))
    return pl.pallas_call(
        partial(_ffn_kernel, hidden=hidden, apply_final=apply_final),
        grid=(n_tok // FFN_TILE,),
        in_specs=[row, _layer(norm_g, layer), _layer(w_gu, layer), _layer(w_down, layer),
                  _resident((1, d))],
        out_specs=row,
        out_shape=jax.ShapeDtypeStruct(h2.shape, F32),
        compiler_params=_params(1),
        name="ffn",
    )(h2, norm_g, w_gu, w_down, final_g.reshape(1, d))


def _sgu_kernel(h_ref, g_ref, win_ref, ws_ref, bst_ref, lng_ref, lnb_ref, wout_ref, o_ref,
                gate_ref):
    width = wout_ref.shape[0]
    h = h_ref[...]
    hn = _rmsnorm(h, g_ref[...]).astype(BF16)
    z = jax.nn.gelu(_mm(hn, win_ref[...]))
    u = z[:, :width]
    v = z[:, width:]
    mu = jnp.mean(v, axis=-1, keepdims=True)
    vc = v - mu
    var = jnp.mean(vc * vc, axis=-1, keepdims=True)
    vn = (vc * lax.rsqrt(var + EPS) * lng_ref[...] + lnb_ref[...]).astype(BF16)

    t_idx = lax.broadcasted_iota(jnp.int32, (CHUNK, CHUNK), 0)
    s_idx = lax.broadcasted_iota(jnp.int32, (CHUNK, CHUNK), 1)
    causal = t_idx >= s_idx
    n_chunks = h.shape[0] // CHUNK
    for g in range(SGU_GROUPS):
        cols = slice(g * SGU_GROUP_DIM, (g + 1) * SGU_GROUP_DIM)
        w = jnp.where(causal, ws_ref[g], 0.0).astype(BF16)
        bias = bst_ref[:, g:g + 1]
        for c in range(n_chunks):
            rows = slice(c * CHUNK, (c + 1) * CHUNK)
            mixed = _mm(w, vn[rows, cols]) + bias
            gate_ref[rows, cols] = (u[rows, cols] * mixed).astype(BF16)
    o_ref[...] = h + _mm(gate_ref[...], wout_ref[...])


def _sgu(h2, layer, sgu_layer, norm_g, w_in, w_s, b_s_t, ln_g, ln_b, w_out):
    n_tok, d = h2.shape
    width = w_out.shape[1]
    row = pl.BlockSpec((SGU_TILE, d), lambda i: (i, 0))
    return pl.pallas_call(
        _sgu_kernel,
        grid=(n_tok // SGU_TILE,),
        in_specs=[row, _layer(norm_g, layer), _layer(w_in, sgu_layer), _layer(w_s, sgu_layer),
                  _layer(b_s_t, sgu_layer), _layer(ln_g, sgu_layer), _layer(ln_b, sgu_layer),
                  _layer(w_out, sgu_layer)],
        out_specs=row,
        out_shape=jax.ShapeDtypeStruct(h2.shape, F32),
        scratch_shapes=[pltpu.VMEM((SGU_TILE, width), BF16)],
        compiler_params=_params(1),
        name="sgu",
    )(h2, norm_g, w_in, w_s, b_s_t, ln_g, ln_b, w_out)


def _rope_kernel(pos_ref, inv_ref, cos_ref, sin_ref):
    pos = pos_ref[...].astype(F32)
    inv = inv_ref[...]
    eye = (lax.broadcasted_iota(jnp.int32, (LANES, LANES), 0)
           == lax.broadcasted_iota(jnp.int32, (LANES, LANES), 1))
    for r in range(SUBLANES):
        col = jnp.sum(jnp.where(eye, pos[r:r + 1, :], 0.0), axis=-1, keepdims=True)
        ang = col * inv
        cos_ref[r * LANES:(r + 1) * LANES, :] = jnp.cos(ang)
        sin_ref[r * LANES:(r + 1) * LANES, :] = jnp.sin(ang)


def _rope_tables(positions):
    n_tok = positions.size
    inv = (1.0 / (ROPE_BASE ** (jnp.arange(ROPE_HALF, dtype=F32) / ROPE_HALF))).reshape(1, ROPE_HALF)
    table = jax.ShapeDtypeStruct((n_tok, ROPE_HALF), F32)
    out_row = pl.BlockSpec((ROPE_ROWS, ROPE_HALF), lambda i: (i, 0))
    return pl.pallas_call(
        _rope_kernel,
        grid=(n_tok // ROPE_ROWS,),
        in_specs=[pl.BlockSpec((SUBLANES, LANES), lambda i: (i, 0)), _resident((1, ROPE_HALF))],
        out_specs=[out_row, out_row],
        out_shape=[table, table],
        compiler_params=_params(1),
        name="rope_tables",
    )(positions.reshape(n_tok // LANES, LANES), inv)


def _retention_decays():
    log_gamma = np.log1p(-(2.0 ** (-5.0 - np.arange(RET_HEADS, dtype=np.float32)))).astype(np.float32)
    idx = np.arange(RET_CHUNK, dtype=np.float32)
    diff = idx[:, None] - idx[None, :]
    inner = np.where(diff >= 0, np.exp(log_gamma[:, None, None] * np.maximum(diff, 0.0)), 0.0)
    cross = np.exp(log_gamma[:, None] * (idx + 1.0))
    state = np.exp(log_gamma[:, None] * (RET_CHUNK - 1.0 - idx))
    chunk = np.exp(log_gamma * RET_CHUNK)
    return (inner.astype(np.float32), cross.T.astype(np.float32).copy(),
            state.T.astype(np.float32).copy(), [float(x) for x in chunk.astype(np.float32)])


def _ret_kernel(h_ref, cos_ref, sin_ref, g_ref, win_ref, gng_ref, wout_ref,
                dinner_ref, dcross_ref, dstate_ref, o_ref,
                state_ref, q_ref, k_ref, ks_ref, v_ref, ret_ref, *, chunk_decay):
    @pl.when(pl.program_id(1) == 0)
    def _():
        state_ref[...] = jnp.zeros_like(state_ref)

    h = h_ref[0]
    tile = h.shape[0]
    n_chunks = tile // RET_CHUNK
    hn = _rmsnorm(h, g_ref[...]).astype(BF16)
    cos = cos_ref[...]
    sin = sin_ref[...]

    def rotate(x, hd):
        x1 = x[:, hd * RET_QK_DIM:hd * RET_QK_DIM + ROPE_HALF]
        x2 = x[:, hd * RET_QK_DIM + ROPE_HALF:(hd + 1) * RET_QK_DIM]
        return x1 * cos - x2 * sin, x1 * sin + x2 * cos

    q = _mm(hn, win_ref[:, :RET_QK_WIDTH])
    k = _mm(hn, win_ref[:, RET_QK_WIDTH:2 * RET_QK_WIDTH])
    k_scale = RET_QK_DIM ** -0.5
    for hd in range(RET_HEADS):
        lo = hd * RET_QK_DIM
        q1, q2 = rotate(q, hd)
        q_ref[:, lo:lo + ROPE_HALF] = q1.astype(BF16)
        q_ref[:, lo + ROPE_HALF:lo + RET_QK_DIM] = q2.astype(BF16)
        k1, k2 = rotate(k, hd)
        k1 = k1 * k_scale
        k2 = k2 * k_scale
        k_ref[:, lo:lo + ROPE_HALF] = k1.astype(BF16)
        k_ref[:, lo + ROPE_HALF:lo + RET_QK_DIM] = k2.astype(BF16)
        sd = dstate_ref[:, hd:hd + 1]
        for c in range(n_chunks):
            rows = slice(c * RET_CHUNK, (c + 1) * RET_CHUNK)
            ks_ref[rows, lo:lo + ROPE_HALF] = (k1[rows] * sd).astype(BF16)
            ks_ref[rows, lo + ROPE_HALF:lo + RET_QK_DIM] = (k2[rows] * sd).astype(BF16)
    v_ref[...] = _mm(hn, win_ref[:, 2 * RET_QK_WIDTH:2 * RET_QK_WIDTH + RET_V_WIDTH]).astype(BF16)

    for hd in range(RET_HEADS):
        qk = slice(hd * RET_QK_DIM, (hd + 1) * RET_QK_DIM)
        vv = slice(hd * RET_V_DIM, (hd + 1) * RET_V_DIM)
        cd = dcross_ref[:, hd:hd + 1]
        for c in range(n_chunks):
            rows = slice(c * RET_CHUNK, (c + 1) * RET_CHUNK)
            qc = q_ref[rows, qk]
            vc = v_ref[rows, vv]
            scores = lax.dot_general(qc, k_ref[rows, qk], (((1,), (1,)), ((), ())),
                                     preferred_element_type=F32) * dinner_ref[hd]
            inner = _mm(scores.astype(BF16), vc)
            st = state_ref[hd]
            cross = _mm(qc, st.astype(BF16)) * cd
            ret_ref[rows, vv] = inner + cross
            state_ref[hd] = st * chunk_decay[hd] + lax.dot_general(
                ks_ref[rows, qk], vc, (((0,), (0,)), ((), ())), preferred_element_type=F32)

    gate = _mm(hn, win_ref[:, 2 * RET_QK_WIDTH + RET_V_WIDTH:])
    for hd in range(RET_HEADS):
        vv = slice(hd * RET_V_DIM, (hd + 1) * RET_V_DIM)
        o = ret_ref[:, vv]
        o = o * lax.rsqrt(jnp.mean(o * o, axis=-1, keepdims=True) + EPS) * gng_ref[:, vv]
        v_ref[:, vv] = (o * jax.nn.silu(gate[:, vv])).astype(BF16)
    o_ref[0] = h + _mm(v_ref[...], wout_ref[...])


def _retention(h, layer, ret_layer, cos, sin, norm_g, w_in, gn_g, w_out):
    b, s, d = h.shape
    dinner, dcross_t, dstate_t, chunk_decay = _retention_decays()
    tiles_per_seq = s // RET_TILE
    row = pl.BlockSpec((1, RET_TILE, d), lambda i, j: (i, j, 0))
    rope = pl.BlockSpec((RET_TILE, ROPE_HALF), lambda i, j: (i * tiles_per_seq + j, 0))
    return pl.pallas_call(
        partial(_ret_kernel, chunk_decay=chunk_decay),
        grid=(b, tiles_per_seq),
        in_specs=[row, rope, rope, _layer(norm_g, layer), _layer(w_in, ret_layer),
                  _layer(gn_g, ret_layer), _layer(w_out, ret_layer),
                  _resident(dinner.shape), _resident(dcross_t.shape), _resident(dstate_t.shape)],
        out_specs=row,
        out_shape=jax.ShapeDtypeStruct(h.shape, F32),
        scratch_shapes=[
            pltpu.VMEM((RET_HEADS, RET_QK_DIM, RET_V_DIM), F32),
            pltpu.VMEM((RET_TILE, RET_QK_WIDTH), BF16),
            pltpu.VMEM((RET_TILE, RET_QK_WIDTH), BF16),
            pltpu.VMEM((RET_TILE, RET_QK_WIDTH), BF16),
            pltpu.VMEM((RET_TILE, RET_V_WIDTH), BF16),
            pltpu.VMEM((RET_TILE, RET_V_WIDTH), F32),
        ],
        compiler_params=_params(2),
        name="retention",
    )(h, cos, sin, norm_g, w_in, gn_g, w_out,
      jnp.asarray(dinner), jnp.asarray(dcross_t), jnp.asarray(dstate_t))


def kernel(x, positions, mix_norm_g, ffn_norm_g, final_norm_g, sgu_w_in, sgu_w_s, sgu_b_s,
           sgu_ln_g, sgu_ln_b, sgu_w_out, ret_w_in, ret_gn_g, ret_w_out, ffn_w_gu, ffn_w_down):
    b, s, d = x.shape
    depth = mix_norm_g.shape[0]
    assert d == D_MODEL and s % RET_TILE == 0 and (b * s) % ROPE_ROWS == 0
    cos, sin = _rope_tables(positions)
    mix_g, ffn_g = _rows(mix_norm_g), _rows(ffn_norm_g)
    sgu_in, sgu_out = sgu_w_in.astype(BF16), sgu_w_out.astype(BF16)
    sgu_b_t = jnp.swapaxes(sgu_b_s, 1, 2)
    sgu_g, sgu_b = _rows(sgu_ln_g), _rows(sgu_ln_b)
    ret_in, ret_out, ret_g = ret_w_in.astype(BF16), ret_w_out.astype(BF16), _rows(ret_gn_g)
    ffn_gu, ffn_down = ffn_w_gu.astype(BF16), ffn_w_down.astype(BF16)
    h = x
    for i in range(depth):
        j = i // 2
        if i % 2 == 0:
            h = _sgu(h.reshape(b * s, d), i, j, mix_g, sgu_in, sgu_w_s, sgu_b_t, sgu_g, sgu_b,
                     sgu_out)
        else:
            h = _retention(h.reshape(b, s, d), i, j, cos, sin, mix_g, ret_in, ret_g, ret_out)
        h = _ffn(h.reshape(b * s, d), i, ffn_g, ffn_gu, ffn_down, final_norm_g,
                 apply_final=(i == depth - 1))
    return h.reshape(b, s, d)
```
